```python
import math
import jax, jax.numpy as jnp
from jax import lax
import numpy as np

D_MODEL = 1024
BATCH = 8
SEQ = 4096
DEPTH = 2
DEC_BATCH = 4
DEC_SEQ = 4096
PAST_LEN = 128

HEAD_DIM = 64
HEADS_PER_GROUP = 4
WINDOWS = (128, 512, 2048)
DILATIONS = (1, 4, 16)
N_GROUPS = 3
RADII = tuple(w // (2 * r) for w, r in zip(WINDOWS, DILATIONS))
ATTN_QKV_WIDTH = N_GROUPS * HEADS_PER_GROUP * HEAD_DIM
ATTN_OUT_WIDTH = HEADS_PER_GROUP * HEAD_DIM
CONV_WIDTH = D_MODEL
CONV_KERNEL = 3
D_FF = 4 * D_MODEL
ROPE_THETA = 10000.0
NORM_EPS = 1e-6
NEG_INF = -1e30
SPLIT_SIZES = (ATTN_QKV_WIDTH, ATTN_QKV_WIDTH, ATTN_QKV_WIDTH,
               CONV_WIDTH, CONV_WIDTH, CONV_WIDTH, D_MODEL, D_MODEL)
IN_WIDTH = sum(SPLIT_SIZES)

kernel_name = "hybrid_dilated_attn_shortconv_encoder"


def rms_norm(x, g):
    xf = x.astype(jnp.float32)
    y = xf * lax.rsqrt(jnp.mean(xf * xf, axis=-1, keepdims=True) + NORM_EPS)
    return (y * g.astype(jnp.float32)).astype(x.dtype)


def rope_tables(seq):
    inv = ROPE_THETA ** (-jnp.arange(0, HEAD_DIM, 2, dtype=jnp.float32) / HEAD_DIM)
    ang = jnp.arange(seq, dtype=jnp.float32)[:, None] * inv[None, :]
    ang = jnp.concatenate([ang, ang], axis=-1)
    return jnp.cos(ang), jnp.sin(ang)


def apply_rope(t, cos, sin):
    tf = t.astype(jnp.float32)
    half = HEAD_DIM // 2
    rot = jnp.concatenate([-tf[..., half:], tf[..., :half]], axis=-1)
    c = cos[None, :, None, None, :]
    s = sin[None, :, None, None, :]
    return (tf * c + rot * s).astype(t.dtype)


def dilated_window_attention(q, k, v, dilation, radius):
    b, s, h, d = q.shape
    L = s // dilation
    blk = radius
    nblk = -(-L // blk)
    lp = nblk * blk

    def to_classes(t):
        return t.reshape(b, L, dilation, h, d)

    qc = jnp.pad(to_classes(q), ((0, 0), (0, lp - L), (0, 0), (0, 0), (0, 0)))
    kv_pad = ((0, 0), (blk, lp - L + blk), (0, 0), (0, 0), (0, 0))
    kc = jnp.pad(to_classes(k), kv_pad)
    vc = jnp.pad(to_classes(v), kv_pad)
    qb = qc.reshape(b, nblk, blk, dilation, h, d)
    kb = kc.reshape(b, nblk + 2, blk, dilation, h, d)
    vb = vc.reshape(b, nblk + 2, blk, dilation, h, d)
    kw = jnp.concatenate([kb[:, :-2], kb[:, 1:-1], kb[:, 2:]], axis=2)
    vw = jnp.concatenate([vb[:, :-2], vb[:, 1:-1], vb[:, 2:]], axis=2)

    scores = jnp.einsum('bnqrhd,bnkrhd->bnrhqk', qb, kw,
                        preferred_element_type=jnp.float32) * (d ** -0.5)
    q_pos = jnp.arange(nblk)[:, None] * blk + jnp.arange(blk)[None, :]
    k_pos = jnp.arange(nblk)[:, None] * blk - blk + jnp.arange(3 * blk)[None, :]
    kp = k_pos[:, None, :]
    valid = (jnp.abs(q_pos[:, :, None] - kp) <= radius) & (kp >= 0) & (kp < L)
    scores = jnp.where(valid[None, :, None, None], scores, NEG_INF)
    m = jnp.max(scores, axis=-1, keepdims=True)
    p = jnp.exp(scores - m)
    l = jnp.sum(p, axis=-1, keepdims=True)
    o = jnp.einsum('bnrhqk,bnkrhd->bnrhqd', p, vw.astype(jnp.float32)) / l
    lse = (m + jnp.log(l))[..., 0]
    o = jnp.transpose(o, (0, 1, 4, 2, 3, 5)).reshape(b, lp, dilation, h, d)[:, :L]
    lse = jnp.transpose(lse, (0, 1, 4, 2, 3)).reshape(b, lp, dilation, h)[:, :L]
    return o.reshape(b, s, h, d), lse.reshape(b, s, h)


def encoder_layer(x, g_mix_pre, g_mix_post, g_mlp_pre, g_mlp_post, w_in, w_attn_out,
                  conv_w, w_conv_out, w_out, w_up, w_down, cos, sin):
    b, s, _ = x.shape
    hn = rms_norm(x, g_mix_pre)
    z = hn @ w_in
    q, k, v, conv_b, conv_c, conv_h, gate_a, gate_c = jnp.split(
        z, np.cumsum(SPLIT_SIZES)[:-1].tolist(), axis=-1)

    grp = (b, s, N_GROUPS, HEADS_PER_GROUP, HEAD_DIM)
    q = apply_rope(q.reshape(grp), cos, sin)
    k = apply_rope(k.reshape(grp), cos, sin)
    v = v.reshape(grp)
    outs, lses = [], []
    for g in range(N_GROUPS):
        o_g, lse_g = dilated_window_attention(q[:, :, g], k[:, :, g], v[:, :, g],
                                              DILATIONS[g], RADII[g])
        outs.append(o_g)
        lses.append(lse_g)
    wts = jax.nn.softmax(jnp.stack(lses, axis=0), axis=0)
    attn = jnp.sum(wts[..., None] * jnp.stack(outs, axis=0), axis=0)
    y_a = attn.astype(x.dtype).reshape(b, s, ATTN_OUT_WIDTH) @ w_attn_out

    u = conv_c * conv_h
    up = jnp.pad(u, ((0, 0), (1, 1), (0, 0)))
    conv = up[:, :-2] * conv_w[0] + up[:, 1:-1] * conv_w[1] + up[:, 2:] * conv_w[2]
    y_c = (conv_b * conv) @ w_conv_out

    mixed = (jax.nn.sigmoid(gate_a) * y_a + jax.nn.sigmoid(gate_c) * y_c) @ w_out
    x = x + rms_norm(mixed, g_mix_post)

    hm = rms_norm(x, g_mlp_pre)
    f = jnp.square(jax.nn.relu(hm @ w_up)) @ w_down
    return x + rms_norm(f, g_mlp_post)


def run_trunk(x, g_mix_pre, g_mix_post, g_mlp_pre, g_mlp_post, w_in, w_attn_out,
              conv_w, w_conv_out, w_out, w_up, w_down):
    cos, sin = rope_tables(x.shape[1])
    for i in range(DEPTH):
        x = encoder_layer(x, g_mix_pre[i], g_mix_post[i], g_mlp_pre[i], g_mlp_post[i],
                          w_in[i], w_attn_out[i], conv_w[i], w_conv_out[i], w_out[i],
                          w_up[i], w_down[i], cos, sin)
    return x


def setup_inputs(seed: int = 0) -> dict:
    key = jax.random.key(seed)
    ks = jax.random.split(key, 16)
    f32 = jnp.float32

    def nrm(k, shape, scale):
        return jax.random.normal(k, shape, f32) * scale

    def gain(k):
        return 1.0 + 0.02 * jax.random.normal(k, (DEPTH, D_MODEL), f32)

    return {
        "x_prompt": jax.random.normal(ks[0], (BATCH, SEQ, D_MODEL), f32),
        "x_sample": jax.random.normal(ks[1], (DEC_BATCH, DEC_SEQ, D_MODEL), f32),
        "g_mix_pre": gain(ks[2]),
        "g_mix_post": gain(ks[3]),
        "g_mlp_pre": gain(ks[4]),
        "g_mlp_post": gain(ks[5]),
        "w_in": nrm(ks[6], (DEPTH, D_MODEL, IN_WIDTH), D_MODEL ** -0.5),
        "w_attn_out": nrm(ks[7], (DEPTH, ATTN_OUT_WIDTH, D_MODEL), ATTN_OUT_WIDTH ** -0.5),
        "conv_w": nrm(ks[8], (DEPTH, CONV_KERNEL, CONV_WIDTH), CONV_KERNEL ** -0.5),
        "w_conv_out": nrm(ks[9], (DEPTH, CONV_WIDTH, D_MODEL), CONV_WIDTH ** -0.5),
        "w_out": nrm(ks[10], (DEPTH, D_MODEL, D_MODEL), D_MODEL ** -0.5),
        "w_up": nrm(ks[11], (DEPTH, D_MODEL, D_FF), D_MODEL ** -0.5),
        "w_down": nrm(ks[12], (DEPTH, D_FF, D_MODEL), D_FF ** -0.5),
    }


def reference(x_prompt, x_sample, g_mix_pre, g_mix_post, g_mlp_pre, g_mlp_post, w_in,
              w_attn_out, conv_w, w_conv_out, w_out, w_up, w_down):
    y_prompt = run_trunk(x_prompt, g_mix_pre, g_mix_post, g_mlp_pre, g_mlp_post, w_in,
                         w_attn_out, conv_w, w_conv_out, w_out, w_up, w_down)
    y_sample = run_trunk(x_sample, g_mix_pre, g_mix_post, g_mlp_pre, g_mlp_post, w_in,
                         w_attn_out, conv_w, w_conv_out, w_out, w_up, w_down)
    return (y_prompt, y_sample)
```

```python
import functools

import jax
import jax.numpy as jnp
from jax import lax
from jax.experimental import pallas as pl
from jax.experimental.pallas import tpu as pltpu

D_MODEL = 1024
HEAD_DIM = 64
HEADS_PER_GROUP = 4
N_GROUPS = 3
DILATIONS = (1, 4, 16)
RADIUS = 64
GROUP_WIDTH = HEADS_PER_GROUP * HEAD_DIM
QKV_WIDTH = N_GROUPS * GROUP_WIDTH
D_FF = 4 * D_MODEL
ROPE_THETA = 10000.0
NORM_EPS = 1e-6
NEG_INF = -1e30

LANES = 128
BF16_SUBLANES = 16

TOKEN_TILE = 512
Q_BLOCK = 128
KEY_WINDOW = Q_BLOCK + 2 * RADIUS
FF_CHUNK = 1024
VMEM_LIMIT_BYTES = 56 * 1024 * 1024

F32 = jnp.float32
BF16 = jnp.bfloat16


def _rms_norm(x, g):
    return x * lax.rsqrt(jnp.mean(x * x, axis=-1, keepdims=True) + NORM_EPS) * g


def _sigmoid(x):
    return 1.0 / (1.0 + jnp.exp(-x))


def _resident(shape):
    return pl.BlockSpec(shape, lambda *_: (0,) * len(shape), pipeline_mode=pl.Buffered(1))


def _in_proj_kernel(x_ref, g_ref, cos_ref, sin_ref, w_ref,
                    qkv_ref, cb_ref, u_ref, ga_ref, gc_ref):
    hn = _rms_norm(x_ref[...], g_ref[...]).astype(BF16)
    cos = cos_ref[...]
    sin = sin_ref[...]
    tm = cos.shape[0]
    first_half = (lax.broadcasted_iota(jnp.int32, (tm, LANES), 1) % HEAD_DIM) < (HEAD_DIM // 2)

    def proj(c0, width):
        return jnp.dot(hn, w_ref[:, c0:c0 + width], preferred_element_type=F32)

    def rope(z):
        rot = jnp.where(first_half,
                        pltpu.roll(z, LANES - HEAD_DIM // 2, 1),
                        pltpu.roll(z, HEAD_DIM // 2, 1))
        return z * cos + rot * sin

    for t, scale in ((0, HEAD_DIM ** -0.5), (1, 1.0)):
        z = proj(t * QKV_WIDTH, QKV_WIDTH)
        for g in range(N_GROUPS):
            for half in range(GROUP_WIDTH // LANES):
                c0 = g * GROUP_WIDTH + half * LANES
                r = rope(z[:, c0:c0 + LANES]) * scale
                qkv_ref[3 * g + t, :, half * LANES:(half + 1) * LANES] = r.astype(BF16)
    z = proj(2 * QKV_WIDTH, QKV_WIDTH)
    for g in range(N_GROUPS):
        qkv_ref[3 * g + 2] = z[:, g * GROUP_WIDTH:(g + 1) * GROUP_WIDTH].astype(BF16)

    base = 3 * QKV_WIDTH
    cb_ref[...] = proj(base, D_MODEL).astype(BF16)
    u_ref[...] = (proj(base + D_MODEL, D_MODEL) * proj(base + 2 * D_MODEL, D_MODEL)).astype(BF16)
    ga_ref[...] = _sigmoid(proj(base + 3 * D_MODEL, D_MODEL)).astype(BF16)
    gc_ref[...] = _sigmoid(proj(base + 4 * D_MODEL, D_MODEL)).astype(BF16)


def _in_proj(x, g, cos, sin, w, seq):
    ntok = x.shape[0]
    tm = TOKEN_TILE
    tiles_per_seq = seq // tm
    tok = lambda i: (i, 0)
    pos = lambda i: (i % tiles_per_seq, 0)
    wide = jax.ShapeDtypeStruct((ntok, D_MODEL), BF16)
    return pl.pallas_call(
        _in_proj_kernel,
        grid=(ntok // tm,),
        in_specs=[
            pl.BlockSpec((tm, D_MODEL), tok),
            _resident((1, D_MODEL)),
            pl.BlockSpec((tm, LANES), pos),
            pl.BlockSpec((tm, LANES), pos),
            _resident(w.shape),
        ],
        out_specs=[
            pl.BlockSpec((3 * N_GROUPS, tm, GROUP_WIDTH), lambda i: (0, i, 0)),
            pl.BlockSpec((tm, D_MODEL), tok),
            pl.BlockSpec((tm, D_MODEL), tok),
            pl.BlockSpec((tm, D_MODEL), tok),
            pl.BlockSpec((tm, D_MODEL), tok),
        ],
        out_shape=[jax.ShapeDtypeStruct((3 * N_GROUPS, ntok, GROUP_WIDTH), BF16),
                   wide, wide, wide, wide],
        compiler_params=pltpu.CompilerParams(
            dimension_semantics=("arbitrary",), vmem_limit_bytes=VMEM_LIMIT_BYTES),
        name="in_proj",
    )(x, g, cos, sin, w)


def _attention_kernel(q_ref, k_ref, v_ref, o_ref, lse_ref):
    length = q_ref.shape[0]
    rows = HEADS_PER_GROUP * Q_BLOCK
    lane_head = lax.broadcasted_iota(jnp.int32, (Q_BLOCK, GROUP_WIDTH), 1) // HEAD_DIM
    rel = (lax.broadcasted_iota(jnp.int32, (rows, KEY_WINDOW), 1)
           - (lax.broadcasted_iota(jnp.int32, (rows, KEY_WINDOW), 0) % Q_BLOCK))

    def step(i, carry):
        q0 = pl.multiple_of(i * Q_BLOCK, Q_BLOCK)
        k0 = pl.multiple_of(jnp.clip(q0 - RADIUS, 0, length - KEY_WINDOW), RADIUS)
        qb = q_ref[pl.ds(q0, Q_BLOCK), :]
        kb = k_ref[pl.ds(k0, KEY_WINDOW), :]
        vb = v_ref[pl.ds(k0, KEY_WINDOW), :]
        qs = jnp.concatenate(
            [jnp.where(lane_head == h, qb, jnp.zeros_like(qb)) for h in range(HEADS_PER_GROUP)],
            axis=0)
        s = lax.dot_general(qs, kb, (((1,), (1,)), ((), ())), preferred_element_type=F32)
        valid = jnp.abs(rel + (k0 - q0)) <= RADIUS
        s = jnp.where(valid, s, NEG_INF)
        m = jnp.max(s, axis=-1, keepdims=True)
        p = jnp.exp(s - m)
        l = jnp.sum(p, axis=-1, keepdims=True)
        pv = jnp.dot(p.astype(BF16), vb, preferred_element_type=F32) / l
        lse = m + jnp.log(l)
        o = pv[0:Q_BLOCK]
        lse_b = jnp.broadcast_to(lse[0:Q_BLOCK], (Q_BLOCK, GROUP_WIDTH))
        for h in range(1, HEADS_PER_GROUP):
            sel = lane_head == h
            o = jnp.where(sel, pv[h * Q_BLOCK:(h + 1) * Q_BLOCK], o)
            lse_b = jnp.where(
                sel, jnp.broadcast_to(lse[h * Q_BLOCK:(h + 1) * Q_BLOCK], (Q_BLOCK, GROUP_WIDTH)),
                lse_b)
        o_ref[pl.ds(q0, Q_BLOCK), :] = o.astype(BF16)
        lse_ref[pl.ds(q0, Q_BLOCK), :] = lse_b
        return carry

    lax.fori_loop(0, length // Q_BLOCK, step, 0)


def _attention(qkv, group, batch, seq):
    r = DILATIONS[group]
    ntok = qkv.shape[1]
    length = seq // r
    view = qkv.reshape(3 * N_GROUPS, ntok // r, r * GROUP_WIDTH)

    def operand(t):
        return pl.BlockSpec((None, length, GROUP_WIDTH), lambda b, c: (3 * group + t, b, c))

    out_block = pl.BlockSpec((length, GROUP_WIDTH), lambda b, c: (b, c))
    o, lse = pl.pallas_call(
        _attention_kernel,
        grid=(batch, r),
        in_specs=[operand(0), operand(1), operand(2)],
        out_specs=[out_block, out_block],
        out_shape=[jax.ShapeDtypeStruct((ntok // r, r * GROUP_WIDTH), BF16),
                   jax.ShapeDtypeStruct((ntok // r, r * GROUP_WIDTH), F32)],
        compiler_params=pltpu.CompilerParams(
            dimension_semantics=("arbitrary", "arbitrary"), vmem_limit_bytes=VMEM_LIMIT_BYTES),
        name=f"attention_g{group}",
    )(view, view, view)
    return o.reshape(ntok, GROUP_WIDTH), lse.reshape(ntok, GROUP_WIDTH)


def _post_kernel(tiles_per_seq,
                 x_ref, o0_ref, o1_ref, o2_ref, l0_ref, l1_ref, l2_ref,
                 cb_ref, u_ref, up_ref, un_ref, ga_ref, gc_ref,
                 wa_ref, cw_ref, wc_ref, wo_ref, wup_ref, wdn_ref,
                 gpost_ref, gpre_ref, gmlp_ref, out_ref):
    tm = x_ref.shape[0]
    i = pl.program_id(0)

    l0, l1, l2 = l0_ref[...], l1_ref[...], l2_ref[...]
    lmax = jnp.maximum(jnp.maximum(l0, l1), l2)
    e0, e1, e2 = jnp.exp(l0 - lmax), jnp.exp(l1 - lmax), jnp.exp(l2 - lmax)
    attn = (e0 * o0_ref[...].astype(F32) + e1 * o1_ref[...].astype(F32)
            + e2 * o2_ref[...].astype(F32)) / (e0 + e1 + e2)
    y_a = jnp.dot(attn.astype(BF16), wa_ref[...], preferred_element_type=F32)

    u = u_ref[...].astype(F32)
    at_seq_start = (i % tiles_per_seq) == 0
    at_seq_end = (i % tiles_per_seq) == tiles_per_seq - 1
    prev_row = jnp.where(at_seq_start, 0.0, up_ref[...].astype(F32)[BF16_SUBLANES - 1:, :])
    next_row = jnp.where(at_seq_end, 0.0, un_ref[...].astype(F32)[0:1, :])
    row = lax.broadcasted_iota(jnp.int32, (tm, D_MODEL), 0)
    u_prev = jnp.where(row == 0, prev_row, pltpu.roll(u, 1, 0))
    u_next = jnp.where(row == tm - 1, next_row, pltpu.roll(u, tm - 1, 0))
    cw = cw_ref[...]
    conv = u_prev * cw[0:1, :] + u * cw[1:2, :] + u_next * cw[2:3, :]
    y_c = jnp.dot((cb_ref[...].astype(F32) * conv).astype(BF16), wc_ref[...],
                  preferred_element_type=F32)

    gated = ga_ref[...].astype(F32) * y_a + gc_ref[...].astype(F32) * y_c
    mixed = jnp.dot(gated.astype(BF16), wo_ref[...], preferred_element_type=F32)
    x1 = x_ref[...] + _rms_norm(mixed, gpost_ref[...])

    hm = _rms_norm(x1, gpre_ref[...]).astype(BF16)
    f = jnp.zeros((tm, D_MODEL), F32)
    for c in range(D_FF // FF_CHUNK):
        h = jnp.dot(hm, wup_ref[:, c * FF_CHUNK:(c + 1) * FF_CHUNK], preferred_element_type=F32)
        h = jnp.maximum(h, 0.0)
        f = f + jnp.dot((h * h).astype(BF16), wdn_ref[c * FF_CHUNK:(c + 1) * FF_CHUNK, :],
                        preferred_element_type=F32)
    out_ref[...] = x1 + _rms_norm(f, gmlp_ref[...])


def _post(x, outs, lses, cb, u, ga, gc, wa, cw, wc, wo, wup, wdn, gpost, gpre, gmlp, seq):
    ntok = x.shape[0]
    tm = TOKEN_TILE
    halo = BF16_SUBLANES
    tiles_per_seq = seq // tm
    tok = lambda i: (i, 0)
    wide = pl.BlockSpec((tm, D_MODEL), tok)
    narrow = pl.BlockSpec((tm, GROUP_WIDTH), tok)
    prev_halo = pl.BlockSpec((halo, D_MODEL), lambda i: (jnp.maximum(i * (tm // halo) - 1, 0), 0))
    next_halo = pl.BlockSpec(
        (halo, D_MODEL), lambda i: (jnp.minimum((i + 1) * (tm // halo), ntok // halo - 1), 0))
    return pl.pallas_call(
        functools.partial(_post_kernel, tiles_per_seq),
        grid=(ntok // tm,),
        in_specs=[wide, narrow, narrow, narrow, narrow, narrow, narrow,
                  wide, wide, prev_halo, next_halo, wide, wide,
                  _resident(wa.shape), _resident(cw.shape), _resident(wc.shape),
                  _resident(wo.shape), _resident(wup.shape), _resident(wdn.shape),
                  _resident((1, D_MODEL)), _resident((1, D_MODEL)), _resident((1, D_MODEL))],
        out_specs=wide,
        out_shape=jax.ShapeDtypeStruct((ntok, D_MODEL), F32),
        compiler_params=pltpu.CompilerParams(
            dimension_semantics=("arbitrary",), vmem_limit_bytes=VMEM_LIMIT_BYTES),
        name="post",
    )(x, *outs, *lses, cb, u, u, u, ga, gc, wa, cw, wc, wo, wup, wdn, gpost, gpre, gmlp)


def _rope_tables(seq):
    inv = ROPE_THETA ** (-jnp.arange(0, HEAD_DIM, 2, dtype=F32) / HEAD_DIM)
    ang = jnp.arange(seq, dtype=F32)[:, None] * inv[None, :]
    cos = jnp.cos(ang)
    sin = jnp.sin(ang)
    cos = jnp.concatenate([cos, cos, cos, cos], axis=-1)
    sin = jnp.concatenate([-sin, sin, -sin, sin], axis=-1)
    return cos, sin


def kernel(x_prompt, x_sample, g_mix_pre, g_mix_post, g_mlp_pre, g_mlp_post, w_in, w_attn_out,
           conv_w, w_conv_out, w_out, w_up, w_down):
    depth = w_in.shape[0]
    row = lambda a: a.reshape(1, D_MODEL)
    layers = [dict(
        w_in=w_in[i].astype(BF16), wa=w_attn_out[i].astype(BF16), cw=conv_w[i],
        wc=w_conv_out[i].astype(BF16), wo=w_out[i].astype(BF16), wup=w_up[i].astype(BF16),
        wdn=w_down[i].astype(BF16), g_pre=row(g_mix_pre[i]), g_post=row(g_mix_post[i]),
        g_mlp_pre=row(g_mlp_pre[i]), g_mlp_post=row(g_mlp_post[i])) for i in range(depth)]

    def trunk(x3):
        batch, seq, _ = x3.shape
        assert seq % TOKEN_TILE == 0 and seq // max(DILATIONS) >= KEY_WINDOW
        cos, sin = _rope_tables(seq)
        x = x3.reshape(batch * seq, D_MODEL)
        for p in layers:
            qkv, cb, u, ga, gc = _in_proj(x, p["g_pre"], cos, sin, p["w_in"], seq)
            outs, lses = zip(*[_attention(qkv, g, batch, seq) for g in range(N_GROUPS)])
            x = _post(x, outs, lses, cb, u, ga, gc, p["wa"], p["cw"], p["wc"], p["wo"],
                      p["wup"], p["wdn"], p["g_post"], p["g_mlp_pre"], p["g_mlp_post"], seq)
        return x.reshape(batch, seq, D_MODEL)

    return trunk(x_prompt), trunk(x_sample)
```

```python
import functools

import jax
import jax.numpy as jnp
from jax import lax
from jax.experimental import pallas as pl
from jax.experimental.pallas import tpu as pltpu

D_MODEL = 1024
HEAD_DIM = 64
HEADS_PER_GROUP = 4
N_GROUPS = 3
DILATIONS = (1, 4, 16)
RADIUS = 64
GROUP_WIDTH = HEADS_PER_GROUP * HEAD_DIM
QKV_WIDTH = N_GROUPS * GROUP_WIDTH
D_FF = 4 * D_MODEL
ROPE_THETA = 10000.0
NORM_EPS = 1e-6
NEG_INF = -1e30

LANES = 128
BF16_SUBLANES = 16

TOKEN_TILE = 512
Q_BLOCK = 128
KEY_WINDOW = Q_BLOCK + 2 * RADIUS
FF_CHUNK = 1024
VMEM_LIMIT_BYTES = 56 * 1024 * 1024

F32 = jnp.float32
BF16 = jnp.bfloat16


def _rms_norm(x, g):
    return x * lax.rsqrt(jnp.mean(x * x, axis=-1, keepdims=True) + NORM_EPS) * g


def _sigmoid(x):
    return 1.0 / (1.0 + jnp.exp(-x))


def _resident(shape):
    return pl.BlockSpec(shape, lambda *_: (0,) * len(shape), pipeline_mode=pl.Buffered(1))


def _in_proj_kernel(x_ref, g_ref, cos_ref, sin_ref, w_ref,
                    qkv0_ref, qkv1_ref, qkv2_ref, cb_ref, u_ref, ga_ref, gc_ref, perm_ref):
    hn = _rms_norm(x_ref[...], g_ref[...]).astype(BF16)
    cos = cos_ref[...]
    sin = sin_ref[...]
    tm = cos.shape[0]
    first_half = (lax.broadcasted_iota(jnp.int32, (tm, LANES), 1) % HEAD_DIM) < (HEAD_DIM // 2)
    qkv_refs = (qkv0_ref, qkv1_ref, qkv2_ref)

    def proj(c0, width):
        return jnp.dot(hn, w_ref[:, c0:c0 + width], preferred_element_type=F32)

    def rope(z):
        rot = jnp.where(first_half,
                        pltpu.roll(z, LANES - HEAD_DIM // 2, 1),
                        pltpu.roll(z, HEAD_DIM // 2, 1))
        return z * cos + rot * sin

    def emit(t, g, half, val):
        lanes = slice(half * LANES, (half + 1) * LANES)
        r = DILATIONS[g]
        if r == 1:
            qkv_refs[g][t, :, lanes] = val.astype(BF16)
            return
        slab = ((g - 1) * 3 + t) * 2 + half
        perm_ref[slab] = val
        for c in range(r):
            qkv_refs[g][t, c, :, lanes] = (
                perm_ref[slab, pl.ds(c, tm // r, stride=r), :].astype(BF16))

    for t, scale in ((0, HEAD_DIM ** -0.5), (1, 1.0), (2, None)):
        z = proj(t * QKV_WIDTH, QKV_WIDTH)
        for g in range(N_GROUPS):
            for half in range(GROUP_WIDTH // LANES):
                c0 = g * GROUP_WIDTH + half * LANES
                val = z[:, c0:c0 + LANES]
                if scale is not None:
                    val = rope(val) * scale
                emit(t, g, half, val)

    base = 3 * QKV_WIDTH
    cb_ref[...] = proj(base, D_MODEL).astype(BF16)
    u_ref[...] = (proj(base + D_MODEL, D_MODEL) * proj(base + 2 * D_MODEL, D_MODEL)).astype(BF16)
    ga_ref[...] = _sigmoid(proj(base + 3 * D_MODEL, D_MODEL)).astype(BF16)
    gc_ref[...] = _sigmoid(proj(base + 4 * D_MODEL, D_MODEL)).astype(BF16)


def _class_major_block(r, tm, tiles_per_seq, lead):
    n = len(lead)
    return pl.BlockSpec(
        lead + (None, r, tm // r, GROUP_WIDTH),
        lambda i: (0,) * n + (i // tiles_per_seq, 0, i % tiles_per_seq, 0))


def _in_proj(x, g, cos, sin, w, batch, seq):
    ntok = x.shape[0]
    tm = TOKEN_TILE
    tiles_per_seq = seq // tm
    tok = lambda i: (i, 0)
    pos = lambda i: (i % tiles_per_seq, 0)
    wide = jax.ShapeDtypeStruct((ntok, D_MODEL), BF16)
    qkv_specs = [pl.BlockSpec((3, tm, GROUP_WIDTH), lambda i: (0, i, 0))]
    qkv_shapes = [jax.ShapeDtypeStruct((3, ntok, GROUP_WIDTH), BF16)]
    for r in DILATIONS[1:]:
        qkv_specs.append(_class_major_block(r, tm, tiles_per_seq, (3,)))
        qkv_shapes.append(jax.ShapeDtypeStruct((3, batch, r, seq // r, GROUP_WIDTH), BF16))
    n_slabs = (N_GROUPS - 1) * 3 * (GROUP_WIDTH // LANES)
    return pl.pallas_call(
        _in_proj_kernel,
        grid=(ntok // tm,),
        in_specs=[
            pl.BlockSpec((tm, D_MODEL), tok),
            _resident((1, D_MODEL)),
            pl.BlockSpec((tm, LANES), pos),
            pl.BlockSpec((tm, LANES), pos),
            _resident(w.shape),
        ],
        out_specs=qkv_specs + [pl.BlockSpec((tm, D_MODEL), tok)] * 4,
        out_shape=qkv_shapes + [wide] * 4,
        scratch_shapes=[pltpu.VMEM((n_slabs, tm, LANES), F32)],
        compiler_params=pltpu.CompilerParams(
            dimension_semantics=("arbitrary",), vmem_limit_bytes=VMEM_LIMIT_BYTES),
        name="in_proj",
    )(x, g, cos, sin, w)


def _attention_kernel(q_ref, k_ref, v_ref, o_ref, lse_ref):
    length = q_ref.shape[0]
    rows = HEADS_PER_GROUP * Q_BLOCK
    lane_head = lax.broadcasted_iota(jnp.int32, (Q_BLOCK, GROUP_WIDTH), 1) // HEAD_DIM
    rel = (lax.broadcasted_iota(jnp.int32, (rows, KEY_WINDOW), 1)
           - (lax.broadcasted_iota(jnp.int32, (rows, KEY_WINDOW), 0) % Q_BLOCK))

    def step(i, carry):
        q0 = pl.multiple_of(i * Q_BLOCK, Q_BLOCK)
        k0 = pl.multiple_of(jnp.clip(q0 - RADIUS, 0, length - KEY_WINDOW), RADIUS)
        qb = q_ref[pl.ds(q0, Q_BLOCK), :]
        kb = k_ref[pl.ds(k0, KEY_WINDOW), :]
        vb = v_ref[pl.ds(k0, KEY_WINDOW), :]
        qs = jnp.concatenate(
            [jnp.where(lane_head == h, qb, jnp.zeros_like(qb)) for h in range(HEADS_PER_GROUP)],
            axis=0)
        s = lax.dot_general(qs, kb, (((1,), (1,)), ((), ())), preferred_element_type=F32)
        valid = jnp.abs(rel + (k0 - q0)) <= RADIUS
        s = jnp.where(valid, s, NEG_INF)
        m = jnp.max(s, axis=-1, keepdims=True)
        p = jnp.exp(s - m)
        l = jnp.sum(p, axis=-1, keepdims=True)
        pv = jnp.dot(p.astype(BF16), vb, preferred_element_type=F32) / l
        lse = m + jnp.log(l)
        o = pv[0:Q_BLOCK]
        lse_b = jnp.broadcast_to(lse[0:Q_BLOCK], (Q_BLOCK, GROUP_WIDTH))
        for h in range(1, HEADS_PER_GROUP):
            sel = lane_head == h
            o = jnp.where(sel, pv[h * Q_BLOCK:(h + 1) * Q_BLOCK], o)
            lse_b = jnp.where(
                sel, jnp.broadcast_to(lse[h * Q_BLOCK:(h + 1) * Q_BLOCK], (Q_BLOCK, GROUP_WIDTH)),
                lse_b)
        o_ref[pl.ds(q0, Q_BLOCK), :] = o.astype(BF16)
        lse_ref[pl.ds(q0, Q_BLOCK), :] = lse_b
        return carry

    lax.fori_loop(0, length // Q_BLOCK, step, 0)


def _attention(qkv, group, batch, seq):
    r = DILATIONS[group]
    length = seq // r
    if r == 1:
        operand = lambda t: pl.BlockSpec((None, length, GROUP_WIDTH), lambda b, c: (t, b, 0))
        out_block = pl.BlockSpec((length, GROUP_WIDTH), lambda b, c: (b, 0))
        out_dims = (batch * seq, GROUP_WIDTH)
    else:
        operand = lambda t: pl.BlockSpec((None, None, None, length, GROUP_WIDTH),
                                         lambda b, c: (t, b, c, 0, 0))
        out_block = pl.BlockSpec((None, None, length, GROUP_WIDTH), lambda b, c: (b, c, 0, 0))
        out_dims = (batch, r, length, GROUP_WIDTH)
    return pl.pallas_call(
        _attention_kernel,
        grid=(batch, r),
        in_specs=[operand(0), operand(1), operand(2)],
        out_specs=[out_block, out_block],
        out_shape=[jax.ShapeDtypeStruct(out_dims, BF16), jax.ShapeDtypeStruct(out_dims, F32)],
        compiler_params=pltpu.CompilerParams(
            dimension_semantics=("arbitrary", "arbitrary"), vmem_limit_bytes=VMEM_LIMIT_BYTES),
        name=f"attention_g{group}",
    )(qkv, qkv, qkv)


def _post_kernel(tiles_per_seq,
                 x_ref, o0_ref, o1_ref, o2_ref, l0_ref, l1_ref, l2_ref,
                 cb_ref, u_ref, up_ref, un_ref, ga_ref, gc_ref,
                 wa_ref, cw_ref, wc_ref, wo_ref, wup_ref, wdn_ref,
                 gpost_ref, gpre_ref, gmlp_ref, out_ref, mix_ref):
    tm = x_ref.shape[0]
    i = pl.program_id(0)
    n_half = GROUP_WIDTH // LANES

    def token_order(ref, g, slab0):
        r = DILATIONS[g]
        if r == 1:
            val = ref[...].astype(F32)
            return [val[:, h * LANES:(h + 1) * LANES] for h in range(n_half)]
        for c in range(r):
            val = ref[c].astype(F32)
            for h in range(n_half):
                mix_ref[slab0 + h, pl.ds(c, tm // r, stride=r), :] = (
                    val[:, h * LANES:(h + 1) * LANES])
        return [mix_ref[slab0 + h] for h in range(n_half)]

    o_refs = (o0_ref, o1_ref, o2_ref)
    l_refs = (l0_ref, l1_ref, l2_ref)
    outs = [token_order(o_refs[g], g, 2 * (g - 1) * n_half) for g in range(N_GROUPS)]
    lses = [token_order(l_refs[g], g, (2 * (g - 1) + 1) * n_half) for g in range(N_GROUPS)]
    halves = []
    for h in range(n_half):
        l0, l1, l2 = lses[0][h], lses[1][h], lses[2][h]
        lmax = jnp.maximum(jnp.maximum(l0, l1), l2)
        e0, e1, e2 = jnp.exp(l0 - lmax), jnp.exp(l1 - lmax), jnp.exp(l2 - lmax)
        halves.append((e0 * outs[0][h] + e1 * outs[1][h] + e2 * outs[2][h]) / (e0 + e1 + e2))
    attn = jnp.concatenate(halves, axis=1)
    y_a = jnp.dot(attn.astype(BF16), wa_ref[...], preferred_element_type=F32)

    u = u_ref[...].astype(F32)
    at_seq_start = (i % tiles_per_seq) == 0
    at_seq_end = (i % tiles_per_seq) == tiles_per_seq - 1
    prev_row = jnp.where(at_seq_start, 0.0, up_ref[...].astype(F32)[BF16_SUBLANES - 1:, :])
    next_row = jnp.where(at_seq_end, 0.0, un_ref[...].astype(F32)[0:1, :])
    row = lax.broadcasted_iota(jnp.int32, (tm, D_MODEL), 0)
    u_prev = jnp.where(row == 0, prev_row, pltpu.roll(u, 1, 0))
    u_next = jnp.where(row == tm - 1, next_row, pltpu.roll(u, tm - 1, 0))
    cw = cw_ref[...]
    conv = u_prev * cw[0:1, :] + u * cw[1:2, :] + u_next * cw[2:3, :]
    y_c = jnp.dot((cb_ref[...].astype(F32) * conv).astype(BF16), wc_ref[...],
                  preferred_element_type=F32)

    gated = ga_ref[...].astype(F32) * y_a + gc_ref[...].astype(F32) * y_c
    mixed = jnp.dot(gated.astype(BF16), wo_ref[...], preferred_element_type=F32)
    x1 = x_ref[...] + _rms_norm(mixed, gpost_ref[...])

    hm = _rms_norm(x1, gpre_ref[...]).astype(BF16)
    f = jnp.zeros((tm, D_MODEL), F32)
    for c in range(D_FF // FF_CHUNK):
        h = jnp.dot(hm, wup_ref[:, c * FF_CHUNK:(c + 1) * FF_CHUNK], preferred_element_type=F32)
        h = jnp.maximum(h, 0.0)
        f = f + jnp.dot((h * h).astype(BF16), wdn_ref[c * FF_CHUNK:(c + 1) * FF_CHUNK, :],
                        preferred_element_type=F32)
    out_ref[...] = x1 + _rms_norm(f, gmlp_ref[...])


def _post(x, outs, lses, cb, u, ga, gc, wa, cw, wc, wo, wup, wdn, gpost, gpre, gmlp, seq):
    ntok = x.shape[0]
    tm = TOKEN_TILE
    halo = BF16_SUBLANES
    tiles_per_seq = seq // tm
    tok = lambda i: (i, 0)
    wide = pl.BlockSpec((tm, D_MODEL), tok)
    narrow = [pl.BlockSpec((tm, GROUP_WIDTH), tok)]
    narrow += [_class_major_block(r, tm, tiles_per_seq, ()) for r in DILATIONS[1:]]
    n_slabs = (N_GROUPS - 1) * 2 * (GROUP_WIDTH // LANES)
    prev_halo = pl.BlockSpec((halo, D_MODEL), lambda i: (jnp.maximum(i * (tm // halo) - 1, 0), 0))
    next_halo = pl.BlockSpec(
        (halo, D_MODEL), lambda i: (jnp.minimum((i + 1) * (tm // halo), ntok // halo - 1), 0))
    return pl.pallas_call(
        functools.partial(_post_kernel, tiles_per_seq),
        grid=(ntok // tm,),
        in_specs=[wide] + narrow + narrow + [
                  wide, wide, prev_halo, next_halo, wide, wide,
                  _resident(wa.shape), _resident(cw.shape), _resident(wc.shape),
                  _resident(wo.shape), _resident(wup.shape), _resident(wdn.shape),
                  _resident((1, D_MODEL)), _resident((1, D_MODEL)), _resident((1, D_MODEL))],
        out_specs=wide,
        out_shape=jax.ShapeDtypeStruct((ntok, D_MODEL), F32),
        scratch_shapes=[pltpu.VMEM((n_slabs, tm, LANES), F32)],
        compiler_params=pltpu.CompilerParams(
            dimension_semantics=("arbitrary",), vmem_limit_bytes=VMEM_LIMIT_BYTES),
        name="post",
    )(x, *outs, *lses, cb, u, u, u, ga, gc, wa, cw, wc, wo, wup, wdn, gpost, gpre, gmlp)


def _rope_tables(seq):
    inv = ROPE_THETA ** (-jnp.arange(0, HEAD_DIM, 2, dtype=F32) / HEAD_DIM)
    ang = jnp.arange(seq, dtype=F32)[:, None] * inv[None, :]
    cos = jnp.cos(ang)
    sin = jnp.sin(ang)
    cos = jnp.concatenate([cos, cos, cos, cos], axis=-1)
    sin = jnp.concatenate([-sin, sin, -sin, sin], axis=-1)
    return cos, sin


def kernel(x_prompt, x_sample, g_mix_pre, g_mix_post, g_mlp_pre, g_mlp_post, w_in, w_attn_out,
           conv_w, w_conv_out, w_out, w_up, w_down):
    depth = w_in.shape[0]
    row = lambda a: a.reshape(1, D_MODEL)
    layers = [dict(
        w_in=w_in[i].astype(BF16), wa=w_attn_out[i].astype(BF16), cw=conv_w[i],
        wc=w_conv_out[i].astype(BF16), wo=w_out[i].astype(BF16), wup=w_up[i].astype(BF16),
        wdn=w_down[i].astype(BF16), g_pre=row(g_mix_pre[i]), g_post=row(g_mix_post[i]),
        g_mlp_pre=row(g_mlp_pre[i]), g_mlp_post=row(g_mlp_post[i])) for i in range(depth)]

    def trunk(x3):
        batch, seq, _ = x3.shape
        assert seq % TOKEN_TILE == 0 and seq // max(DILATIONS) >= KEY_WINDOW
        cos, sin = _rope_tables(seq)
        x = x3.reshape(batch * seq, D_MODEL)
        for p in layers:
            *qkvs, cb, u, ga, gc = _in_proj(x, p["g_pre"], cos, sin, p["w_in"], batch, seq)
            outs, lses = zip(*[_attention(qkvs[g], g, batch, seq) for g in range(N_GROUPS)])
            x = _post(x, outs, lses, cb, u, ga, gc, p["wa"], p["cw"], p["wc"], p["wo"],
                      p["wup"], p["wdn"], p["g_post"], p["g_mlp_pre"], p["g_mlp_post"], seq)
        return x.reshape(batch, seq, D_MODEL)

    return trunk(x_prompt), trunk(x_sample)
```

```python
import functools

import jax
import jax.numpy as jnp
from jax import lax
from jax.experimental import pallas as pl
from jax.experimental.pallas import tpu as pltpu

D_MODEL = 1024
HEAD_DIM = 64
HEADS_PER_GROUP = 4
N_GROUPS = 3
DILATIONS = (1, 4, 16)
RADIUS = 64
GROUP_WIDTH = HEADS_PER_GROUP * HEAD_DIM
QKV_WIDTH = N_GROUPS * GROUP_WIDTH
D_FF = 4 * D_MODEL
ROPE_THETA = 10000.0
NORM_EPS = 1e-6
NEG_INF = -1e30

LANES = 128
BF16_SUBLANES = 16

TOKEN_TILE = 512
Q_BLOCK = 128
KEY_WINDOW = Q_BLOCK + 2 * RADIUS
ATTN_UNROLL = 4
ATTN_MIN_BLOCKS_PER_STEP = 2 * ATTN_UNROLL
FF_CHUNK = 1024
VMEM_LIMIT_BYTES = 56 * 1024 * 1024

F32 = jnp.float32
BF16 = jnp.bfloat16


def _rms_norm(x, g):
    return x * lax.rsqrt(jnp.mean(x * x, axis=-1, keepdims=True) + NORM_EPS) * g


def _sigmoid(x):
    return 1.0 / (1.0 + jnp.exp(-x))


def _resident(shape):
    return pl.BlockSpec(shape, lambda *_: (0,) * len(shape), pipeline_mode=pl.Buffered(1))


def _in_proj_kernel(x_ref, g_ref, cos_ref, sin_ref, w_ref,
                    qkv0_ref, qkv1_ref, qkv2_ref, cb_ref, u_ref, ga_ref, gc_ref, perm_ref):
    hn = _rms_norm(x_ref[...], g_ref[...]).astype(BF16)
    cos = cos_ref[...]
    sin = sin_ref[...]
    tm = cos.shape[0]
    first_half = (lax.broadcasted_iota(jnp.int32, (tm, LANES), 1) % HEAD_DIM) < (HEAD_DIM // 2)
    qkv_refs = (qkv0_ref, qkv1_ref, qkv2_ref)

    def proj(c0, width):
        return jnp.dot(hn, w_ref[:, c0:c0 + width], preferred_element_type=F32)

    def rope(z):
        rot = jnp.where(first_half,
                        pltpu.roll(z, LANES - HEAD_DIM // 2, 1),
                        pltpu.roll(z, HEAD_DIM // 2, 1))
        return z * cos + rot * sin

    def emit(t, g, half, val):
        lanes = slice(half * LANES, (half + 1) * LANES)
        r = DILATIONS[g]
        if r == 1:
            qkv_refs[g][t, :, lanes] = val.astype(BF16)
            return
        slab = ((g - 1) * 3 + t) * 2 + half
        perm_ref[slab] = val
        for c in range(r):
            qkv_refs[g][t, c, :, lanes] = (
                perm_ref[slab, pl.ds(c, tm // r, stride=r), :].astype(BF16))

    for t, scale in ((0, HEAD_DIM ** -0.5), (1, 1.0), (2, None)):
        z = proj(t * QKV_WIDTH, QKV_WIDTH)
        for g in range(N_GROUPS):
            for half in range(GROUP_WIDTH // LANES):
                c0 = g * GROUP_WIDTH + half * LANES
                val = z[:, c0:c0 + LANES]
                if scale is not None:
                    val = rope(val) * scale
                emit(t, g, half, val)

    base = 3 * QKV_WIDTH
    cb_ref[...] = proj(base, D_MODEL).astype(BF16)
    u_ref[...] = (proj(base + D_MODEL, D_MODEL) * proj(base + 2 * D_MODEL, D_MODEL)).astype(BF16)
    ga_ref[...] = _sigmoid(proj(base + 3 * D_MODEL, D_MODEL)).astype(BF16)
    gc_ref[...] = _sigmoid(proj(base + 4 * D_MODEL, D_MODEL)).astype(BF16)


def _class_major_block(r, tm, tiles_per_seq, lead):
    n = len(lead)
    return pl.BlockSpec(
        lead + (None, r, tm // r, GROUP_WIDTH),
        lambda i: (0,) * n + (i // tiles_per_seq, 0, i % tiles_per_seq, 0))


def _in_proj(x, g, cos, sin, w, batch, seq):
    ntok = x.shape[0]
    tm = TOKEN_TILE
    tiles_per_seq = seq // tm
    tok = lambda i: (i, 0)
    pos = lambda i: (i % tiles_per_seq, 0)
    wide = jax.ShapeDtypeStruct((ntok, D_MODEL), BF16)
    qkv_specs = [pl.BlockSpec((3, tm, GROUP_WIDTH), lambda i: (0, i, 0))]
    qkv_shapes = [jax.ShapeDtypeStruct((3, ntok, GROUP_WIDTH), BF16)]
    for r in DILATIONS[1:]:
        qkv_specs.append(_class_major_block(r, tm, tiles_per_seq, (3,)))
        qkv_shapes.append(jax.ShapeDtypeStruct((3, batch, r, seq // r, GROUP_WIDTH), BF16))
    n_slabs = (N_GROUPS - 1) * 3 * (GROUP_WIDTH // LANES)
    return pl.pallas_call(
        _in_proj_kernel,
        grid=(ntok // tm,),
        in_specs=[
            pl.BlockSpec((tm, D_MODEL), tok),
            _resident((1, D_MODEL)),
            pl.BlockSpec((tm, LANES), pos),
            pl.BlockSpec((tm, LANES), pos),
            _resident(w.shape),
        ],
        out_specs=qkv_specs + [pl.BlockSpec((tm, D_MODEL), tok)] * 4,
        out_shape=qkv_shapes + [wide] * 4,
        scratch_shapes=[pltpu.VMEM((n_slabs, tm, LANES), F32)],
        compiler_params=pltpu.CompilerParams(
            dimension_semantics=("arbitrary",), vmem_limit_bytes=VMEM_LIMIT_BYTES),
        name="in_proj",
    )(x, g, cos, sin, w)


def _attention_kernel(q_ref, k_ref, v_ref, o_ref, lse_ref):
    length = q_ref.shape[-2]
    n_classes = q_ref.shape[0] if len(q_ref.shape) == 3 else 1
    blocks_per_class = length // Q_BLOCK
    rows = HEADS_PER_GROUP * Q_BLOCK
    lane_head = lax.broadcasted_iota(jnp.int32, (Q_BLOCK, GROUP_WIDTH), 1) // HEAD_DIM
    rel = (lax.broadcasted_iota(jnp.int32, (rows, KEY_WINDOW), 1)
           - (lax.broadcasted_iota(jnp.int32, (rows, KEY_WINDOW), 0) % Q_BLOCK))

    def rows_of(ref, cls, start, size):
        if len(ref.shape) == 3:
            return ref.at[cls, pl.ds(start, size), :]
        return ref.at[pl.ds(start, size), :]

    def step(i, carry):
        cls = i // blocks_per_class
        q0 = pl.multiple_of((i % blocks_per_class) * Q_BLOCK, Q_BLOCK)
        k0 = pl.multiple_of(jnp.clip(q0 - RADIUS, 0, length - KEY_WINDOW), RADIUS)
        qb = rows_of(q_ref, cls, q0, Q_BLOCK)[...]
        kb = rows_of(k_ref, cls, k0, KEY_WINDOW)[...]
        vb = rows_of(v_ref, cls, k0, KEY_WINDOW)[...]
        qs = jnp.concatenate(
            [jnp.where(lane_head == h, qb, jnp.zeros_like(qb)) for h in range(HEADS_PER_GROUP)],
            axis=0)
        s = lax.dot_general(qs, kb, (((1,), (1,)), ((), ())), preferred_element_type=F32)
        valid = jnp.abs(rel + (k0 - q0)) <= RADIUS
        s = jnp.where(valid, s, NEG_INF)
        m = jnp.max(s, axis=-1, keepdims=True)
        p = jnp.exp(s - m)
        l = jnp.sum(p, axis=-1, keepdims=True)
        pv = jnp.dot(p.astype(BF16), vb, preferred_element_type=F32) * (1.0 / l)
        lse = m + jnp.log(l)
        o = pv[0:Q_BLOCK]
        lse_b = jnp.broadcast_to(lse[0:Q_BLOCK], (Q_BLOCK, GROUP_WIDTH))
        for h in range(1, HEADS_PER_GROUP):
            sel = lane_head == h
            o = jnp.where(sel, pv[h * Q_BLOCK:(h + 1) * Q_BLOCK], o)
            lse_b = jnp.where(
                sel, jnp.broadcast_to(lse[h * Q_BLOCK:(h + 1) * Q_BLOCK], (Q_BLOCK, GROUP_WIDTH)),
                lse_b)
        rows_of(o_ref, cls, q0, Q_BLOCK)[...] = o.astype(BF16)
        rows_of(lse_ref, cls, q0, Q_BLOCK)[...] = lse_b
        return carry

    n_blocks = n_classes * blocks_per_class
    lax.fori_loop(0, n_blocks, step, 0, unroll=min(ATTN_UNROLL, n_blocks))


def _attention(qkv, group, batch, seq):
    r = DILATIONS[group]
    length = seq // r
    if r == 1:
        nc = 1
        operand = lambda t: pl.BlockSpec((None, length, GROUP_WIDTH), lambda b, c: (t, b, 0))
        out_block = pl.BlockSpec((length, GROUP_WIDTH), lambda b, c: (b, 0))
        out_dims = (batch * seq, GROUP_WIDTH)
    else:
        nc = min(r, max(1, ATTN_MIN_BLOCKS_PER_STEP * Q_BLOCK // length))
        operand = lambda t: pl.BlockSpec((None, None, nc, length, GROUP_WIDTH),
                                         lambda b, c: (t, b, c, 0, 0))
        out_block = pl.BlockSpec((None, nc, length, GROUP_WIDTH), lambda b, c: (b, c, 0, 0))
        out_dims = (batch, r, length, GROUP_WIDTH)
    return pl.pallas_call(
        _attention_kernel,
        grid=(batch, r // nc),
        in_specs=[operand(0), operand(1), operand(2)],
        out_specs=[out_block, out_block],
        out_shape=[jax.ShapeDtypeStruct(out_dims, BF16), jax.ShapeDtypeStruct(out_dims, F32)],
        compiler_params=pltpu.CompilerParams(
            dimension_semantics=("arbitrary", "arbitrary"), vmem_limit_bytes=VMEM_LIMIT_BYTES),
        name=f"attention_g{group}",
    )(qkv, qkv, qkv)


def _post_kernel(tiles_per_seq,
                 x_ref, o0_ref, o1_ref, o2_ref, l0_ref, l1_ref, l2_ref,
                 cb_ref, u_ref, up_ref, un_ref, ga_ref, gc_ref,
                 wa_ref, cw_ref, wc_ref, wo_ref, wup_ref, wdn_ref,
                 gpost_ref, gpre_ref, gmlp_ref, out_ref, mix_ref):
    tm = x_ref.shape[0]
    i = pl.program_id(0)
    n_half = GROUP_WIDTH // LANES

    def token_order(ref, g, slab0):
        r = DILATIONS[g]
        if r == 1:
            val = ref[...].astype(F32)
            return [val[:, h * LANES:(h + 1) * LANES] for h in range(n_half)]
        for c in range(r):
            val = ref[c].astype(F32)
            for h in range(n_half):
                mix_ref[slab0 + h, pl.ds(c, tm // r, stride=r), :] = (
                    val[:, h * LANES:(h + 1) * LANES])
        return [mix_ref[slab0 + h] for h in range(n_half)]

    o_refs = (o0_ref, o1_ref, o2_ref)
    l_refs = (l0_ref, l1_ref, l2_ref)
    outs = [token_order(o_refs[g], g, 2 * (g - 1) * n_half) for g in range(N_GROUPS)]
    lses = [token_order(l_refs[g], g, (2 * (g - 1) + 1) * n_half) for g in range(N_GROUPS)]
    halves = []
    for h in range(n_half):
        l0, l1, l2 = lses[0][h], lses[1][h], lses[2][h]
        lmax = jnp.maximum(jnp.maximum(l0, l1), l2)
        e0, e1, e2 = jnp.exp(l0 - lmax), jnp.exp(l1 - lmax), jnp.exp(l2 - lmax)
        halves.append((e0 * outs[0][h] + e1 * outs[1][h] + e2 * outs[2][h]) / (e0 + e1 + e2))
    attn = jnp.concatenate(halves, axis=1)
    y_a = jnp.dot(attn.astype(BF16), wa_ref[...], preferred_element_type=F32)

    u = u_ref[...].astype(F32)
    at_seq_start = (i % tiles_per_seq) == 0
    at_seq_end = (i % tiles_per_seq) == tiles_per_seq - 1
    prev_row = jnp.where(at_seq_start, 0.0, up_ref[...].astype(F32)[BF16_SUBLANES - 1:, :])
    next_row = jnp.where(at_seq_end, 0.0, un_ref[...].astype(F32)[0:1, :])
    row = lax.broadcasted_iota(jnp.int32, (tm, D_MODEL), 0)
    u_prev = jnp.where(row == 0, prev_row, pltpu.roll(u, 1, 0))
    u_next = jnp.where(row == tm - 1, next_row, pltpu.roll(u, tm - 1, 0))
    cw = cw_ref[...]
    conv = u_prev * cw[0:1, :] + u * cw[1:2, :] + u_next * cw[2:3, :]
    y_c = jnp.dot((cb_ref[...].astype(F32) * conv).astype(BF16), wc_ref[...],
                  preferred_element_type=F32)

    gated = ga_ref[...].astype(F32) * y_a + gc_ref[...].astype(F32) * y_c
    mixed = jnp.dot(gated.astype(BF16), wo_ref[...], preferred_element_type=F32)
    x1 = x_ref[...] + _rms_norm(mixed, gpost_ref[...])

    hm = _rms_norm(x1, gpre_ref[...]).astype(BF16)
    f = jnp.zeros((tm, D_MODEL), F32)
    for c in range(D_FF // FF_CHUNK):
        h = jnp.dot(hm, wup_ref[:, c * FF_CHUNK:(c + 1) * FF_CHUNK], preferred_element_type=F32)
        h = jnp.maximum(h, 0.0)
        f = f + jnp.dot((h * h).astype(BF16), wdn_ref[c * FF_CHUNK:(c + 1) * FF_CHUNK, :],
                        preferred_element_type=F32)
    out_ref[...] = x1 + _rms_norm(f, gmlp_ref[...])


def _post(x, outs, lses, cb, u, ga, gc, wa, cw, wc, wo, wup, wdn, gpost, gpre, gmlp, seq):
    ntok = x.shape[0]
    tm = TOKEN_TILE
    halo = BF16_SUBLANES
    tiles_per_seq = seq // tm
    tok = lambda i: (i, 0)
    wide = pl.BlockSpec((tm, D_MODEL), tok)
    narrow = [pl.BlockSpec((tm, GROUP_WIDTH), tok)]
    narrow += [_class_major_block(r, tm, tiles_per_seq, ()) for r in DILATIONS[1:]]
    n_slabs = (N_GROUPS - 1) * 2 * (GROUP_WIDTH // LANES)
    prev_halo = pl.BlockSpec((halo, D_MODEL), lambda i: (jnp.maximum(i * (tm // halo) - 1, 0), 0))
    next_halo = pl.BlockSpec(
        (halo, D_MODEL), lambda i: (jnp.minimum((i + 1) * (tm // halo), ntok // halo - 1), 0))
    return pl.pallas_call(
        functools.partial(_post_kernel, tiles_per_seq),
        grid=(ntok // tm,),
        in_specs=[wide] + narrow + narrow + [
                  wide, wide, prev_halo, next_halo, wide, wide,
                  _resident(wa.shape), _resident(cw.shape), _resident(wc.shape),
                  _resident(wo.shape), _resident(wup.shape), _resident(wdn.shape),
                  _resident((1, D_MODEL)), _resident((1, D_MODEL)), _resident((1, D_MODEL))],
        out_specs=wide,
        out_shape=jax.ShapeDtypeStruct((ntok, D_MODEL), F32),
        scratch_shapes=[pltpu.VMEM((n_slabs, tm, LANES), F32)],
        compiler_params=pltpu.CompilerParams(
            dimension_semantics=("arbitrary",), vmem_limit_bytes=VMEM_LIMIT_BYTES),
        name="post",
    )(x, *outs, *lses, cb, u, u, u, ga, gc, wa, cw, wc, wo, wup, wdn, gpost, gpre, gmlp)


def _rope_tables(seq):
    inv = ROPE_THETA ** (-jnp.arange(0, HEAD_DIM, 2, dtype=F32) / HEAD_DIM)
    ang = jnp.arange(seq, dtype=F32)[:, None] * inv[None, :]
    cos = jnp.cos(ang)
    sin = jnp.sin(ang)
    cos = jnp.concatenate([cos, cos, cos, cos], axis=-1)
    sin = jnp.concatenate([-sin, sin, -sin, sin], axis=-1)
    return cos, sin


def kernel(x_prompt, x_sample, g_mix_pre, g_mix_post, g_mlp_pre, g_mlp_post, w_in, w_attn_out,
           conv_w, w_conv_out, w_out, w_up, w_down):
    depth = w_in.shape[0]
    row = lambda a: a.reshape(1, D_MODEL)
    layers = [dict(
        w_in=w_in[i].astype(BF16), wa=w_attn_out[i].astype(BF16), cw=conv_w[i],
        wc=w_conv_out[i].astype(BF16), wo=w_out[i].astype(BF16), wup=w_up[i].astype(BF16),
        wdn=w_down[i].astype(BF16), g_pre=row(g_mix_pre[i]), g_post=row(g_mix_post[i]),
        g_mlp_pre=row(g_mlp_pre[i]), g_mlp_post=row(g_mlp_post[i])) for i in range(depth)]

    def trunk(x3):
        batch, seq, _ = x3.shape
        assert seq % TOKEN_TILE == 0 and seq // max(DILATIONS) >= KEY_WINDOW
        cos, sin = _rope_tables(seq)
        x = x3.reshape(batch * seq, D_MODEL)
        for p in layers:
            *qkvs, cb, u, ga, gc = _in_proj(x, p["g_pre"], cos, sin, p["w_in"], batch, seq)
            outs, lses = zip(*[_attention(qkvs[g], g, batch, seq) for g in range(N_GROUPS)])
            x = _post(x, outs, lses, cb, u, ga, gc, p["wa"], p["cw"], p["wc"], p["wo"],
                      p["wup"], p["wdn"], p["g_post"], p["g_mlp_pre"], p["g_mlp_post"], seq)
        return x.reshape(batch, seq, D_MODEL)

    return trunk(x_prompt), trunk(x_sample)
```

```python
import functools

import jax
import jax.numpy as jnp
from jax import lax
from jax.experimental import pallas as pl
from jax.experimental.pallas import tpu as pltpu

D_MODEL = 1024
HEAD_DIM = 64
HEADS_PER_GROUP = 4
N_GROUPS = 3
DILATIONS = (1, 4, 16)
RADIUS = 64
GROUP_WIDTH = HEADS_PER_GROUP * HEAD_DIM
QKV_WIDTH = N_GROUPS * GROUP_WIDTH
D_FF = 4 * D_MODEL
ROPE_THETA = 10000.0
NORM_EPS = 1e-6
NEG_INF = -1e30

LANES = 128
BF16_SUBLANES = 16

TOKEN_TILE = 512
Q_BLOCK = 128
KEY_WINDOW = Q_BLOCK + 2 * RADIUS
N_MASK_CASES = (KEY_WINDOW - Q_BLOCK) // RADIUS + 1
ATTN_UNROLL = 8
ATTN_MIN_BLOCKS_PER_STEP = 2 * ATTN_UNROLL
FF_CHUNK = 1024
VMEM_LIMIT_BYTES = 56 * 1024 * 1024

F32 = jnp.float32
BF16 = jnp.bfloat16


def _rms_norm(x, g):
    return x * lax.rsqrt(jnp.mean(x * x, axis=-1, keepdims=True) + NORM_EPS) * g


def _sigmoid(x):
    return 1.0 / (1.0 + jnp.exp(-x))


def _resident(shape):
    return pl.BlockSpec(shape, lambda *_: (0,) * len(shape), pipeline_mode=pl.Buffered(1))


def _in_proj_kernel(x_ref, g_ref, cos_ref, sin_ref, w_ref,
                    qkv0_ref, qkv1_ref, qkv2_ref, cb_ref, u_ref, ga_ref, gc_ref, perm_ref):
    hn = _rms_norm(x_ref[...], g_ref[...]).astype(BF16)
    cos = cos_ref[...]
    sin = sin_ref[...]
    tm = cos.shape[0]
    first_half = (lax.broadcasted_iota(jnp.int32, (tm, LANES), 1) % HEAD_DIM) < (HEAD_DIM // 2)
    qkv_refs = (qkv0_ref, qkv1_ref, qkv2_ref)

    def proj(c0, width):
        return jnp.dot(hn, w_ref[:, c0:c0 + width], preferred_element_type=F32)

    def rope(z):
        rot = jnp.where(first_half,
                        pltpu.roll(z, LANES - HEAD_DIM // 2, 1),
                        pltpu.roll(z, HEAD_DIM // 2, 1))
        return z * cos + rot * sin

    def emit(t, g, half, val):
        lanes = slice(half * LANES, (half + 1) * LANES)
        r = DILATIONS[g]
        if r == 1:
            qkv_refs[g][t, :, lanes] = val.astype(BF16)
            return
        slab = ((g - 1) * 3 + t) * 2 + half
        perm_ref[slab] = val
        for c in range(r):
            qkv_refs[g][t, c, :, lanes] = (
                perm_ref[slab, pl.ds(c, tm // r, stride=r), :].astype(BF16))

    for t, scale in ((0, HEAD_DIM ** -0.5), (1, 1.0), (2, None)):
        z = proj(t * QKV_WIDTH, QKV_WIDTH)
        for g in range(N_GROUPS):
            for half in range(GROUP_WIDTH // LANES):
                c0 = g * GROUP_WIDTH + half * LANES
                val = z[:, c0:c0 + LANES]
                if scale is not None:
                    val = rope(val) * scale
                emit(t, g, half, val)

    base = 3 * QKV_WIDTH
    cb_ref[...] = proj(base, D_MODEL).astype(BF16)
    u_ref[...] = (proj(base + D_MODEL, D_MODEL) * proj(base + 2 * D_MODEL, D_MODEL)).astype(BF16)
    ga_ref[...] = _sigmoid(proj(base + 3 * D_MODEL, D_MODEL)).astype(BF16)
    gc_ref[...] = _sigmoid(proj(base + 4 * D_MODEL, D_MODEL)).astype(BF16)


def _class_major_block(r, tm, tiles_per_seq, lead):
    n = len(lead)
    return pl.BlockSpec(
        lead + (None, r, tm // r, GROUP_WIDTH),
        lambda i: (0,) * n + (i // tiles_per_seq, 0, i % tiles_per_seq, 0))


def _in_proj(x, g, cos, sin, w, batch, seq):
    ntok = x.shape[0]
    tm = TOKEN_TILE
    tiles_per_seq = seq // tm
    tok = lambda i: (i, 0)
    pos = lambda i: (i % tiles_per_seq, 0)
    wide = jax.ShapeDtypeStruct((ntok, D_MODEL), BF16)
    qkv_specs = [pl.BlockSpec((3, tm, GROUP_WIDTH), lambda i: (0, i, 0))]
    qkv_shapes = [jax.ShapeDtypeStruct((3, ntok, GROUP_WIDTH), BF16)]
    for r in DILATIONS[1:]:
        qkv_specs.append(_class_major_block(r, tm, tiles_per_seq, (3,)))
        qkv_shapes.append(jax.ShapeDtypeStruct((3, batch, r, seq // r, GROUP_WIDTH), BF16))
    n_slabs = (N_GROUPS - 1) * 3 * (GROUP_WIDTH // LANES)
    return pl.pallas_call(
        _in_proj_kernel,
        grid=(ntok // tm,),
        in_specs=[
            pl.BlockSpec((tm, D_MODEL), tok),
            _resident((1, D_MODEL)),
            pl.BlockSpec((tm, LANES), pos),
            pl.BlockSpec((tm, LANES), pos),
            _resident(w.shape),
        ],
        out_specs=qkv_specs + [pl.BlockSpec((tm, D_MODEL), tok)] * 4,
        out_shape=qkv_shapes + [wide] * 4,
        scratch_shapes=[pltpu.VMEM((n_slabs, tm, LANES), F32)],
        compiler_params=pltpu.CompilerParams(
            dimension_semantics=("arbitrary",), vmem_limit_bytes=VMEM_LIMIT_BYTES),
        name="in_proj",
    )(x, g, cos, sin, w)


def _attention_kernel(q_ref, k_ref, v_ref, o_ref, lse_ref, bias_ref):
    length = q_ref.shape[-2]
    n_classes = q_ref.shape[0] if len(q_ref.shape) == 3 else 1
    blocks_per_class = length // Q_BLOCK
    lane_head = lax.broadcasted_iota(jnp.int32, (Q_BLOCK, GROUP_WIDTH), 1) // HEAD_DIM
    rel = (lax.broadcasted_iota(jnp.int32, (Q_BLOCK, KEY_WINDOW), 1)
           - lax.broadcasted_iota(jnp.int32, (Q_BLOCK, KEY_WINDOW), 0))
    for case in range(bias_ref.shape[0]):
        bias_ref[case] = jnp.where(jnp.abs(rel - case * RADIUS) <= RADIUS, 0.0, NEG_INF)

    def rows_of(ref, cls, start, size):
        if len(ref.shape) == 3:
            return ref.at[cls, pl.ds(start, size), :]
        return ref.at[pl.ds(start, size), :]

    def step(i, carry):
        cls = i // blocks_per_class
        q0 = pl.multiple_of((i % blocks_per_class) * Q_BLOCK, Q_BLOCK)
        k0 = pl.multiple_of(jnp.clip(q0 - RADIUS, 0, length - KEY_WINDOW), RADIUS)
        qb = rows_of(q_ref, cls, q0, Q_BLOCK)[...]
        kb = rows_of(k_ref, cls, k0, KEY_WINDOW)[...]
        vb = rows_of(v_ref, cls, k0, KEY_WINDOW)[...]
        qs = jnp.concatenate(
            [jnp.where(lane_head == h, qb, jnp.zeros_like(qb)) for h in range(HEADS_PER_GROUP)],
            axis=0)
        s = lax.dot_general(qs, kb, (((1,), (1,)), ((), ())), preferred_element_type=F32)
        bias = bias_ref[(q0 - k0) // RADIUS]
        s = s + jnp.concatenate([bias] * HEADS_PER_GROUP, axis=0)
        m = jnp.max(s, axis=-1, keepdims=True)
        p = jnp.exp(s - m)
        l = jnp.sum(p, axis=-1, keepdims=True)
        pv = jnp.dot(p.astype(BF16), vb, preferred_element_type=F32) * (1.0 / l)
        lse = m + jnp.log(l)
        o = pv[0:Q_BLOCK]
        lse_b = jnp.broadcast_to(lse[0:Q_BLOCK], (Q_BLOCK, GROUP_WIDTH))
        for h in range(1, HEADS_PER_GROUP):
            sel = lane_head == h
            o = jnp.where(sel, pv[h * Q_BLOCK:(h + 1) * Q_BLOCK], o)
            lse_b = jnp.where(
                sel, jnp.broadcast_to(lse[h * Q_BLOCK:(h + 1) * Q_BLOCK], (Q_BLOCK, GROUP_WIDTH)),
                lse_b)
        rows_of(o_ref, cls, q0, Q_BLOCK)[...] = o.astype(BF16)
        rows_of(lse_ref, cls, q0, Q_BLOCK)[...] = lse_b
        return carry

    n_blocks = n_classes * blocks_per_class
    lax.fori_loop(0, n_blocks, step, 0, unroll=min(ATTN_UNROLL, n_blocks))


def _attention(qkv, group, batch, seq):
    r = DILATIONS[group]
    length = seq // r
    if r == 1:
        nc = 1
        operand = lambda t: pl.BlockSpec((None, length, GROUP_WIDTH), lambda b, c: (t, b, 0))
        out_block = pl.BlockSpec((length, GROUP_WIDTH), lambda b, c: (b, 0))
        out_dims = (batch * seq, GROUP_WIDTH)
    else:
        nc = min(r, max(1, ATTN_MIN_BLOCKS_PER_STEP * Q_BLOCK // length))
        operand = lambda t: pl.BlockSpec((None, None, nc, length, GROUP_WIDTH),
                                         lambda b, c: (t, b, c, 0, 0))
        out_block = pl.BlockSpec((None, nc, length, GROUP_WIDTH), lambda b, c: (b, c, 0, 0))
        out_dims = (batch, r, length, GROUP_WIDTH)
    return pl.pallas_call(
        _attention_kernel,
        grid=(batch, r // nc),
        in_specs=[operand(0), operand(1), operand(2)],
        out_specs=[out_block, out_block],
        out_shape=[jax.ShapeDtypeStruct(out_dims, BF16), jax.ShapeDtypeStruct(out_dims, F32)],
        scratch_shapes=[pltpu.VMEM((N_MASK_CASES, Q_BLOCK, KEY_WINDOW), F32)],
        compiler_params=pltpu.CompilerParams(
            dimension_semantics=("arbitrary", "arbitrary"), vmem_limit_bytes=VMEM_LIMIT_BYTES),
        name=f"attention_g{group}",
    )(qkv, qkv, qkv)


def _post_kernel(tiles_per_seq,
                 x_ref, o0_ref, o1_ref, o2_ref, l0_ref, l1_ref, l2_ref,
                 cb_ref, u_ref, up_ref, un_ref, ga_ref, gc_ref,
                 wa_ref, cw_ref, wc_ref, wo_ref, wup_ref, wdn_ref,
                 gpost_ref, gpre_ref, gmlp_ref, out_ref, mix_ref):
    tm = x_ref.shape[0]
    i = pl.program_id(0)
    n_half = GROUP_WIDTH // LANES

    def token_order(ref, g, slab0):
        r = DILATIONS[g]
        if r == 1:
            val = ref[...].astype(F32)
            return [val[:, h * LANES:(h + 1) * LANES] for h in range(n_half)]
        for c in range(r):
            val = ref[c].astype(F32)
            for h in range(n_half):
                mix_ref[slab0 + h, pl.ds(c, tm // r, stride=r), :] = (
                    val[:, h * LANES:(h + 1) * LANES])
        return [mix_ref[slab0 + h] for h in range(n_half)]

    o_refs = (o0_ref, o1_ref, o2_ref)
    l_refs = (l0_ref, l1_ref, l2_ref)
    outs = [token_order(o_refs[g], g, 2 * (g - 1) * n_half) for g in range(N_GROUPS)]
    lses = [token_order(l_refs[g], g, (2 * (g - 1) + 1) * n_half) for g in range(N_GROUPS)]
    halves = []
    for h in range(n_half):
        l0, l1, l2 = lses[0][h], lses[1][h], lses[2][h]
        lmax = jnp.maximum(jnp.maximum(l0, l1), l2)
        e0, e1, e2 = jnp.exp(l0 - lmax), jnp.exp(l1 - lmax), jnp.exp(l2 - lmax)
        halves.append((e0 * outs[0][h] + e1 * outs[1][h] + e2 * outs[2][h]) / (e0 + e1 + e2))
    attn = jnp.concatenate(halves, axis=1)
    y_a = jnp.dot(attn.astype(BF16), wa_ref[...], preferred_element_type=F32)

    u = u_ref[...].astype(F32)
    at_seq_start = (i % tiles_per_seq) == 0
    at_seq_end = (i % tiles_per_seq) == tiles_per_seq - 1
    prev_row = jnp.where(at_seq_start, 0.0, up_ref[...].astype(F32)[BF16_SUBLANES - 1:, :])
    next_row = jnp.where(at_seq_end, 0.0, un_ref[...].astype(F32)[0:1, :])
    row = lax.broadcasted_iota(jnp.int32, (tm, D_MODEL), 0)
    u_prev = jnp.where(row == 0, prev_row, pltpu.roll(u, 1, 0))
    u_next = jnp.where(row == tm - 1, next_row, pltpu.roll(u, tm - 1, 0))
    cw = cw_ref[...]
    conv = u_prev * cw[0:1, :] + u * cw[1:2, :] + u_next * cw[2:3, :]
    y_c = jnp.dot((cb_ref[...].astype(F32) * conv).astype(BF16), wc_ref[...],
                  preferred_element_type=F32)

    gated = ga_ref[...].astype(F32) * y_a + gc_ref[...].astype(F32) * y_c
    mixed = jnp.dot(gated.astype(BF16), wo_ref[...], preferred_element_type=F32)
    x1 = x_ref[...] + _rms_norm(mixed, gpost_ref[...])

    hm = _rms_norm(x1, gpre_ref[...]).astype(BF16)
    f = jnp.zeros((tm, D_MODEL), F32)
    for c in range(D_FF // FF_CHUNK):
        h = jnp.dot(hm, wup_ref[:, c * FF_CHUNK:(c + 1) * FF_CHUNK], preferred_element_type=F32)
        h = jnp.maximum(h, 0.0)
        f = f + jnp.dot((h * h).astype(BF16), wdn_ref[c * FF_CHUNK:(c + 1) * FF_CHUNK, :],
                        preferred_element_type=F32)
    out_ref[...] = x1 + _rms_norm(f, gmlp_ref[...])


def _post(x, outs, lses, cb, u, ga, gc, wa, cw, wc, wo, wup, wdn, gpost, gpre, gmlp, seq):
    ntok = x.shape[0]
    tm = TOKEN_TILE
    halo = BF16_SUBLANES
    tiles_per_seq = seq // tm
    tok = lambda i: (i, 0)
    wide = pl.BlockSpec((tm, D_MODEL), tok)
    narrow = [pl.BlockSpec((tm, GROUP_WIDTH), tok)]
    narrow += [_class_major_block(r, tm, tiles_per_seq, ()) for r in DILATIONS[1:]]
    n_slabs = (N_GROUPS - 1) * 2 * (GROUP_WIDTH // LANES)
    prev_halo = pl.BlockSpec((halo, D_MODEL), lambda i: (jnp.maximum(i * (tm // halo) - 1, 0), 0))
    next_halo = pl.BlockSpec(
        (halo, D_MODEL), lambda i: (jnp.minimum((i + 1) * (tm // halo), ntok // halo - 1), 0))
    return pl.pallas_call(
        functools.partial(_post_kernel, tiles_per_seq),
        grid=(ntok // tm,),
        in_specs=[wide] + narrow + narrow + [
                  wide, wide, prev_halo, next_halo, wide, wide,
                  _resident(wa.shape), _resident(cw.shape), _resident(wc.shape),
                  _resident(wo.shape), _resident(wup.shape), _resident(wdn.shape),
                  _resident((1, D_MODEL)), _resident((1, D_MODEL)), _resident((1, D_MODEL))],
        out_specs=wide,
        out_shape=jax.ShapeDtypeStruct((ntok, D_MODEL), F32),
        scratch_shapes=[pltpu.VMEM((n_slabs, tm, LANES), F32)],
        compiler_params=pltpu.CompilerParams(
            dimension_semantics=("arbitrary",), vmem_limit_bytes=VMEM_LIMIT_BYTES),
        name="post",
    )(x, *outs, *lses, cb, u, u, u, ga, gc, wa, cw, wc, wo, wup, wdn, gpost, gpre, gmlp)


def _rope_tables(seq):
    inv = ROPE_THETA ** (-jnp.arange(0, HEAD_DIM, 2, dtype=F32) / HEAD_DIM)
    ang = jnp.arange(seq, dtype=F32)[:, None] * inv[None, :]
    cos = jnp.cos(ang)
    sin = jnp.sin(ang)
    cos = jnp.concatenate([cos, cos, cos, cos], axis=-1)
    sin = jnp.concatenate([-sin, sin, -sin, sin], axis=-1)
    return cos, sin


def kernel(x_prompt, x_sample, g_mix_pre, g_mix_post, g_mlp_pre, g_mlp_post, w_in, w_attn_out,
           conv_w, w_conv_out, w_out, w_up, w_down):
    depth = w_in.shape[0]
    row = lambda a: a.reshape(1, D_MODEL)
    layers = [dict(
        w_in=w_in[i].astype(BF16), wa=w_attn_out[i].astype(BF16), cw=conv_w[i],
        wc=w_conv_out[i].astype(BF16), wo=w_out[i].astype(BF16), wup=w_up[i].astype(BF16),
        wdn=w_down[i].astype(BF16), g_pre=row(g_mix_pre[i]), g_post=row(g_mix_post[i]),
        g_mlp_pre=row(g_mlp_pre[i]), g_mlp_post=row(g_mlp_post[i])) for i in range(depth)]

    def trunk(x3):
        batch, seq, _ = x3.shape
        assert seq % TOKEN_TILE == 0 and seq // max(DILATIONS) >= KEY_WINDOW
        cos, sin = _rope_tables(seq)
        x = x3.reshape(batch * seq, D_MODEL)
        for p in layers:
            *qkvs, cb, u, ga, gc = _in_proj(x, p["g_pre"], cos, sin, p["w_in"], batch, seq)
            outs, lses = zip(*[_attention(qkvs[g], g, batch, seq) for g in range(N_GROUPS)])
            x = _post(x, outs, lses, cb, u, ga, gc, p["wa"], p["cw"], p["wc"], p["wo"],
                      p["wup"], p["wdn"], p["g_post"], p["g_mlp_pre"], p["g_mlp_post"], seq)
        return x.reshape(batch, seq, D_MODEL)

    return trunk(x_prompt), trunk(x_sample)
```

```python
import functools

import jax
import jax.numpy as jnp
from jax import lax
from jax.experimental import pallas as pl
from jax.experimental.pallas import tpu as pltpu

D_MODEL = 1024
HEAD_DIM = 64
HEADS_PER_GROUP = 4
N_GROUPS = 3
DILATIONS = (1, 4, 16)
RADIUS = 64
GROUP_WIDTH = HEADS_PER_GROUP * HEAD_DIM
QKV_WIDTH = N_GROUPS * GROUP_WIDTH
D_FF = 4 * D_MODEL
ROPE_THETA = 10000.0
NORM_EPS = 1e-6
NEG_INF = -1e30

LANES = 128
BF16_SUBLANES = 16

TOKEN_TILE = 512
Q_BLOCK = 128
KEY_WINDOW = Q_BLOCK + 2 * RADIUS
N_MASK_CASES = (KEY_WINDOW - Q_BLOCK) // RADIUS + 1
ATTN_UNROLL = 16
ATTN_MIN_BLOCKS_PER_STEP = 2 * ATTN_UNROLL
FF_CHUNK = 1024
VMEM_LIMIT_BYTES = 56 * 1024 * 1024

F32 = jnp.float32
BF16 = jnp.bfloat16


def _rms_norm(x, g):
    return x * lax.rsqrt(jnp.mean(x * x, axis=-1, keepdims=True) + NORM_EPS) * g


def _sigmoid(x):
    return 1.0 / (1.0 + jnp.exp(-x))


def _resident(shape):
    return pl.BlockSpec(shape, lambda *_: (0,) * len(shape), pipeline_mode=pl.Buffered(1))


def _in_proj_kernel(x_ref, g_ref, cos_ref, sin_ref, w_ref,
                    qkv0_ref, qkv1_ref, qkv2_ref, cb_ref, u_ref, ga_ref, gc_ref, perm_ref):
    hn = _rms_norm(x_ref[...], g_ref[...]).astype(BF16)
    cos = cos_ref[...]
    sin = sin_ref[...]
    tm = cos.shape[0]
    first_half = (lax.broadcasted_iota(jnp.int32, (tm, LANES), 1) % HEAD_DIM) < (HEAD_DIM // 2)
    qkv_refs = (qkv0_ref, qkv1_ref, qkv2_ref)

    def proj(c0, width):
        return jnp.dot(hn, w_ref[:, c0:c0 + width], preferred_element_type=F32)

    def rope(z):
        rot = jnp.where(first_half,
                        pltpu.roll(z, LANES - HEAD_DIM // 2, 1),
                        pltpu.roll(z, HEAD_DIM // 2, 1))
        return z * cos + rot * sin

    def emit(t, g, half, val):
        lanes = slice(half * LANES, (half + 1) * LANES)
        r = DILATIONS[g]
        if r == 1:
            qkv_refs[g][t, :, lanes] = val.astype(BF16)
            return
        slab = ((g - 1) * 3 + t) * 2 + half
        perm_ref[slab] = val
        for c in range(r):
            qkv_refs[g][t, c, :, lanes] = (
                perm_ref[slab, pl.ds(c, tm // r, stride=r), :].astype(BF16))

    for t, scale in ((0, HEAD_DIM ** -0.5), (1, 1.0), (2, None)):
        z = proj(t * QKV_WIDTH, QKV_WIDTH)
        for g in range(N_GROUPS):
            for half in range(GROUP_WIDTH // LANES):
                c0 = g * GROUP_WIDTH + half * LANES
                val = z[:, c0:c0 + LANES]
                if scale is not None:
                    val = rope(val) * scale
                emit(t, g, half, val)

    base = 3 * QKV_WIDTH
    cb_ref[...] = proj(base, D_MODEL).astype(BF16)
    u_ref[...] = (proj(base + D_MODEL, D_MODEL) * proj(base + 2 * D_MODEL, D_MODEL)).astype(BF16)
    ga_ref[...] = _sigmoid(proj(base + 3 * D_MODEL, D_MODEL)).astype(BF16)
    gc_ref[...] = _sigmoid(proj(base + 4 * D_MODEL, D_MODEL)).astype(BF16)


def _class_major_block(r, tm, tiles_per_seq, lead):
    n = len(lead)
    return pl.BlockSpec(
        lead + (None, r, tm // r, GROUP_WIDTH),
        lambda i: (0,) * n + (i // tiles_per_seq, 0, i % tiles_per_seq, 0))


def _in_proj(x, g, cos, sin, w, batch, seq):
    ntok = x.shape[0]
    tm = TOKEN_TILE
    tiles_per_seq = seq // tm
    tok = lambda i: (i, 0)
    pos = lambda i: (i % tiles_per_seq, 0)
    wide = jax.ShapeDtypeStruct((ntok, D_MODEL), BF16)
    qkv_specs = [pl.BlockSpec((3, tm, GROUP_WIDTH), lambda i: (0, i, 0))]
    qkv_shapes = [jax.ShapeDtypeStruct((3, ntok, GROUP_WIDTH), BF16)]
    for r in DILATIONS[1:]:
        qkv_specs.append(_class_major_block(r, tm, tiles_per_seq, (3,)))
        qkv_shapes.append(jax.ShapeDtypeStruct((3, batch, r, seq // r, GROUP_WIDTH), BF16))
    n_slabs = (N_GROUPS - 1) * 3 * (GROUP_WIDTH // LANES)
    return pl.pallas_call(
        _in_proj_kernel,
        grid=(ntok // tm,),
        in_specs=[
            pl.BlockSpec((tm, D_MODEL), tok),
            _resident((1, D_MODEL)),
            pl.BlockSpec((tm, LANES), pos),
            pl.BlockSpec((tm, LANES), pos),
            _resident(w.shape),
        ],
        out_specs=qkv_specs + [pl.BlockSpec((tm, D_MODEL), tok)] * 4,
        out_shape=qkv_shapes + [wide] * 4,
        scratch_shapes=[pltpu.VMEM((n_slabs, tm, LANES), F32)],
        compiler_params=pltpu.CompilerParams(
            dimension_semantics=("arbitrary",), vmem_limit_bytes=VMEM_LIMIT_BYTES),
        name="in_proj",
    )(x, g, cos, sin, w)


def _attention_kernel(q_ref, k_ref, v_ref, o_ref, lse_ref, bias_ref):
    length = q_ref.shape[-2]
    n_classes = q_ref.shape[0] if len(q_ref.shape) == 3 else 1
    blocks_per_class = length // Q_BLOCK
    lane_head = lax.broadcasted_iota(jnp.int32, (Q_BLOCK, GROUP_WIDTH), 1) // HEAD_DIM
    rel = (lax.broadcasted_iota(jnp.int32, (Q_BLOCK, KEY_WINDOW), 1)
           - lax.broadcasted_iota(jnp.int32, (Q_BLOCK, KEY_WINDOW), 0))
    for case in range(bias_ref.shape[0]):
        bias_ref[case] = jnp.where(jnp.abs(rel - case * RADIUS) <= RADIUS, 0.0, NEG_INF)

    def rows_of(ref, cls, start, size):
        if len(ref.shape) == 3:
            return ref.at[cls, pl.ds(start, size), :]
        return ref.at[pl.ds(start, size), :]

    def step(i, carry):
        cls = i // blocks_per_class
        q0 = pl.multiple_of((i % blocks_per_class) * Q_BLOCK, Q_BLOCK)
        k0 = pl.multiple_of(jnp.clip(q0 - RADIUS, 0, length - KEY_WINDOW), RADIUS)
        qb = rows_of(q_ref, cls, q0, Q_BLOCK)[...]
        kb = rows_of(k_ref, cls, k0, KEY_WINDOW)[...]
        vb = rows_of(v_ref, cls, k0, KEY_WINDOW)[...]
        qs = jnp.concatenate(
            [jnp.where(lane_head == h, qb, jnp.zeros_like(qb)) for h in range(HEADS_PER_GROUP)],
            axis=0)
        s = lax.dot_general(qs, kb, (((1,), (1,)), ((), ())), preferred_element_type=F32)
        bias = bias_ref[(q0 - k0) // RADIUS]
        s = s + jnp.concatenate([bias] * HEADS_PER_GROUP, axis=0)
        m = jnp.max(s, axis=-1, keepdims=True)
        p = jnp.exp(s - m)
        l = jnp.sum(p, axis=-1, keepdims=True)
        pv = jnp.dot(p.astype(BF16), vb, preferred_element_type=F32) * (1.0 / l)
        lse = m + jnp.log(l)
        o = pv[0:Q_BLOCK]
        lse_b = jnp.broadcast_to(lse[0:Q_BLOCK], (Q_BLOCK, GROUP_WIDTH))
        for h in range(1, HEADS_PER_GROUP):
            sel = lane_head == h
            o = jnp.where(sel, pv[h * Q_BLOCK:(h + 1) * Q_BLOCK], o)
            lse_b = jnp.where(
                sel, jnp.broadcast_to(lse[h * Q_BLOCK:(h + 1) * Q_BLOCK], (Q_BLOCK, GROUP_WIDTH)),
                lse_b)
        rows_of(o_ref, cls, q0, Q_BLOCK)[...] = o.astype(BF16)
        rows_of(lse_ref, cls, q0, Q_BLOCK)[...] = lse_b
        return carry

    n_blocks = n_classes * blocks_per_class
    lax.fori_loop(0, n_blocks, step, 0, unroll=min(ATTN_UNROLL, n_blocks))


def _attention(qkv, group, batch, seq):
    r = DILATIONS[group]
    length = seq // r
    if r == 1:
        nc = 1
        operand = lambda t: pl.BlockSpec((None, length, GROUP_WIDTH), lambda b, c: (t, b, 0))
        out_block = pl.BlockSpec((length, GROUP_WIDTH), lambda b, c: (b, 0))
        out_dims = (batch * seq, GROUP_WIDTH)
    else:
        nc = min(r, max(1, ATTN_MIN_BLOCKS_PER_STEP * Q_BLOCK // length))
        operand = lambda t: pl.BlockSpec((None, None, nc, length, GROUP_WIDTH),
                                         lambda b, c: (t, b, c, 0, 0))
        out_block = pl.BlockSpec((None, nc, length, GROUP_WIDTH), lambda b, c: (b, c, 0, 0))
        out_dims = (batch, r, length, GROUP_WIDTH)
    return pl.pallas_call(
        _attention_kernel,
        grid=(batch, r // nc),
        in_specs=[operand(0), operand(1), operand(2)],
        out_specs=[out_block, out_block],
        out_shape=[jax.ShapeDtypeStruct(out_dims, BF16), jax.ShapeDtypeStruct(out_dims, F32)],
        scratch_shapes=[pltpu.VMEM((N_MASK_CASES, Q_BLOCK, KEY_WINDOW), F32)],
        compiler_params=pltpu.CompilerParams(
            dimension_semantics=("arbitrary", "arbitrary"), vmem_limit_bytes=VMEM_LIMIT_BYTES),
        name=f"attention_g{group}",
    )(qkv, qkv, qkv)


def _post_kernel(tiles_per_seq,
                 x_ref, o0_ref, o1_ref, o2_ref, l0_ref, l1_ref, l2_ref,
                 cb_ref, u_ref, up_ref, un_ref, ga_ref, gc_ref,
                 wa_ref, cw_ref, wc_ref, wo_ref, wup_ref, wdn_ref,
                 gpost_ref, gpre_ref, gmlp_ref, out_ref, mix_ref):
    tm = x_ref.shape[0]
    i = pl.program_id(0)
    n_half = GROUP_WIDTH // LANES

    def token_order(ref, g, slab0):
        r = DILATIONS[g]
        if r == 1:
            val = ref[...].astype(F32)
            return [val[:, h * LANES:(h + 1) * LANES] for h in range(n_half)]
        for c in range(r):
            val = ref[c].astype(F32)
            for h in range(n_half):
                mix_ref[slab0 + h, pl.ds(c, tm // r, stride=r), :] = (
                    val[:, h * LANES:(h + 1) * LANES])
        return [mix_ref[slab0 + h] for h in range(n_half)]

    o_refs = (o0_ref, o1_ref, o2_ref)
    l_refs = (l0_ref, l1_ref, l2_ref)
    outs = [token_order(o_refs[g], g, 2 * (g - 1) * n_half) for g in range(N_GROUPS)]
    lses = [token_order(l_refs[g], g, (2 * (g - 1) + 1) * n_half) for g in range(N_GROUPS)]
    halves = []
    for h in range(n_half):
        l0, l1, l2 = lses[0][h], lses[1][h], lses[2][h]
        lmax = jnp.maximum(jnp.maximum(l0, l1), l2)
        e0, e1, e2 = jnp.exp(l0 - lmax), jnp.exp(l1 - lmax), jnp.exp(l2 - lmax)
        halves.append((e0 * outs[0][h] + e1 * outs[1][h] + e2 * outs[2][h]) / (e0 + e1 + e2))
    attn = jnp.concatenate(halves, axis=1)
    y_a = jnp.dot(attn.astype(BF16), wa_ref[...], preferred_element_type=F32)

    u = u_ref[...].astype(F32)
    at_seq_start = (i % tiles_per_seq) == 0
    at_seq_end = (i % tiles_per_seq) == tiles_per_seq - 1
    prev_row = jnp.where(at_seq_start, 0.0, up_ref[...].astype(F32)[BF16_SUBLANES - 1:, :])
    next_row = jnp.where(at_seq_end, 0.0, un_ref[...].astype(F32)[0:1, :])
    row = lax.broadcasted_iota(jnp.int32, (tm, D_MODEL), 0)
    u_prev = jnp.where(row == 0, prev_row, pltpu.roll(u, 1, 0))
    u_next = jnp.where(row == tm - 1, next_row, pltpu.roll(u, tm - 1, 0))
    cw = cw_ref[...]
    conv = u_prev * cw[0:1, :] + u * cw[1:2, :] + u_next * cw[2:3, :]
    y_c = jnp.dot((cb_ref[...].astype(F32) * conv).astype(BF16), wc_ref[...],
                  preferred_element_type=F32)

    gated = ga_ref[...].astype(F32) * y_a + gc_ref[...].astype(F32) * y_c
    mixed = jnp.dot(gated.astype(BF16), wo_ref[...], preferred_element_type=F32)
    x1 = x_ref[...] + _rms_norm(mixed, gpost_ref[...])

    hm = _rms_norm(x1, gpre_ref[...]).astype(BF16)
    f = jnp.zeros((tm, D_MODEL), F32)
    for c in range(D_FF // FF_CHUNK):
        h = jnp.dot(hm, wup_ref[:, c * FF_CHUNK:(c + 1) * FF_CHUNK], preferred_element_type=F32)
        h = jnp.maximum(h, 0.0)
        f = f + jnp.dot((h * h).astype(BF16), wdn_ref[c * FF_CHUNK:(c + 1) * FF_CHUNK, :],
                        preferred_element_type=F32)
    out_ref[...] = x1 + _rms_norm(f, gmlp_ref[...])


def _post(x, outs, lses, cb, u, ga, gc, wa, cw, wc, wo, wup, wdn, gpost, gpre, gmlp, seq):
    ntok = x.shape[0]
    tm = TOKEN_TILE
    halo = BF16_SUBLANES
    tiles_per_seq = seq // tm
    tok = lambda i: (i, 0)
    wide = pl.BlockSpec((tm, D_MODEL), tok)
    narrow = [pl.BlockSpec((tm, GROUP_WIDTH), tok)]
    narrow += [_class_major_block(r, tm, tiles_per_seq, ()) for r in DILATIONS[1:]]
    n_slabs = (N_GROUPS - 1) * 2 * (GROUP_WIDTH // LANES)
    prev_halo = pl.BlockSpec((halo, D_MODEL), lambda i: (jnp.maximum(i * (tm // halo) - 1, 0), 0))
    next_halo = pl.BlockSpec(
        (halo, D_MODEL), lambda i: (jnp.minimum((i + 1) * (tm // halo), ntok // halo - 1), 0))
    return pl.pallas_call(
        functools.partial(_post_kernel, tiles_per_seq),
        grid=(ntok // tm,),
        in_specs=[wide] + narrow + narrow + [
                  wide, wide, prev_halo, next_halo, wide, wide,
                  _resident(wa.shape), _resident(cw.shape), _resident(wc.shape),
                  _resident(wo.shape), _resident(wup.shape), _resident(wdn.shape),
                  _resident((1, D_MODEL)), _resident((1, D_MODEL)), _resident((1, D_MODEL))],
        out_specs=wide,
        out_shape=jax.ShapeDtypeStruct((ntok, D_MODEL), F32),
        scratch_shapes=[pltpu.VMEM((n_slabs, tm, LANES), F32)],
        compiler_params=pltpu.CompilerParams(
            dimension_semantics=("arbitrary",), vmem_limit_bytes=VMEM_LIMIT_BYTES),
        name="post",
    )(x, *outs, *lses, cb, u, u, u, ga, gc, wa, cw, wc, wo, wup, wdn, gpost, gpre, gmlp)


def _rope_tables(seq):
    inv = ROPE_THETA ** (-jnp.arange(0, HEAD_DIM, 2, dtype=F32) / HEAD_DIM)
    ang = jnp.arange(seq, dtype=F32)[:, None] * inv[None, :]
    cos = jnp.cos(ang)
    sin = jnp.sin(ang)
    cos = jnp.concatenate([cos, cos, cos, cos], axis=-1)
    sin = jnp.concatenate([-sin, sin, -sin, sin], axis=-1)
    return cos, sin


def kernel(x_prompt, x_sample, g_mix_pre, g_mix_post, g_mlp_pre, g_mlp_post, w_in, w_attn_out,
           conv_w, w_conv_out, w_out, w_up, w_down):
    depth = w_in.shape[0]
    row = lambda a: a.reshape(1, D_MODEL)
    layers = [dict(
        w_in=w_in[i].astype(BF16), wa=w_attn_out[i].astype(BF16), cw=conv_w[i],
        wc=w_conv_out[i].astype(BF16), wo=w_out[i].astype(BF16), wup=w_up[i].astype(BF16),
        wdn=w_down[i].astype(BF16), g_pre=row(g_mix_pre[i]), g_post=row(g_mix_post[i]),
        g_mlp_pre=row(g_mlp_pre[i]), g_mlp_post=row(g_mlp_post[i])) for i in range(depth)]

    def trunk(x3):
        batch, seq, _ = x3.shape
        assert seq % TOKEN_TILE == 0 and seq // max(DILATIONS) >= KEY_WINDOW
        cos, sin = _rope_tables(seq)
        x = x3.reshape(batch * seq, D_MODEL)
        for p in layers:
            *qkvs, cb, u, ga, gc = _in_proj(x, p["g_pre"], cos, sin, p["w_in"], batch, seq)
            outs, lses = zip(*[_attention(qkvs[g], g, batch, seq) for g in range(N_GROUPS)])
            x = _post(x, outs, lses, cb, u, ga, gc, p["wa"], p["cw"], p["wc"], p["wo"],
                      p["wup"], p["wdn"], p["g_post"], p["g_mlp_pre"], p["g_mlp_post"], seq)
        return x.reshape(batch, seq, D_MODEL)

    return trunk(x_prompt), trunk(x_sample)
```
